```python
import math
import jax, jax.numpy as jnp
from jax import lax
import numpy as np

D_MODEL = 2048
BATCH = 8
SEQ = 4096
DEPTH = 2

CHUNK = 64
QBLOCK = 128
N_MIXERS = 2
N_MLA_LAYERS = (DEPTH + 1) // 2
N_POOL_LAYERS = DEPTH // 2
N_HEADS = 16
Q_LORA = 512
KV_LORA = 512
NOPE_DIM = 128
ROPE_DIM = 64
V_DIM = 128
QK_HEAD = NOPE_DIM + ROPE_DIM
MLA_IN = Q_LORA + KV_LORA + ROPE_DIM
ROPE_THETA = 10000.0
POOL_WINDOWS = (2, 4, 8, 16)
N_POOL_GROUPS = len(POOL_WINDOWS)
POOL_GROUP_DIM = D_MODEL // N_POOL_GROUPS
N_GROUPS = 8
EXPERTS_PER_GROUP = 8
N_EXPERTS = N_GROUPS * EXPERTS_PER_GROUP
TOP_K_IN_GROUP = 2
D_EXPERT = 512
EXPERT_BLOCK = 256
EPS = 1e-6
NEG_INF = -1e30

kernel_name = "hybrid_mla_pool_hmoe_streaming"


def rms_norm(x, g):
    xf = x.astype(jnp.float32)
    y = xf * lax.rsqrt(jnp.mean(xf * xf, axis=-1, keepdims=True) + EPS)
    return (y * g.astype(jnp.float32)).astype(x.dtype)


def apply_rope(x, cos, sin):
    half = x.shape[-1] // 2
    xf = x.astype(jnp.float32)
    x1, x2 = xf[..., :half], xf[..., half:]
    return jnp.concatenate([x1 * cos - x2 * sin, x2 * cos + x1 * sin], axis=-1).astype(x.dtype)


def mla_mixer(h, positions, w_in, q_lat_norm, kv_lat_norm, w_q_up, w_kv_up, q_norm, k_norm, w_out):
    B, S, _ = h.shape
    c = h @ w_in
    cq = rms_norm(c[..., :Q_LORA], q_lat_norm)
    ckv = rms_norm(c[..., Q_LORA:Q_LORA + KV_LORA], kv_lat_norm)
    k_rope = c[..., Q_LORA + KV_LORA:]
    q = (cq @ w_q_up).reshape(B, S, N_HEADS, QK_HEAD)
    kv = (ckv @ w_kv_up).reshape(B, S, N_HEADS, NOPE_DIM + V_DIM)
    k_nope, v = kv[..., :NOPE_DIM], kv[..., NOPE_DIM:]
    k = jnp.concatenate(
        [k_nope, jnp.broadcast_to(k_rope[:, :, None, :], (B, S, N_HEADS, ROPE_DIM))], axis=-1)
    q = rms_norm(q, q_norm)
    k = rms_norm(k, k_norm)
    inv_freq = 1.0 / (ROPE_THETA ** (jnp.arange(0, ROPE_DIM, 2, dtype=jnp.float32) / ROPE_DIM))
    ang = positions.astype(jnp.float32)[..., None] * inv_freq
    cos = jnp.cos(ang)[:, :, None, :]
    sin = jnp.sin(ang)[:, :, None, :]
    q = jnp.concatenate([q[..., :NOPE_DIM], apply_rope(q[..., NOPE_DIM:], cos, sin)], axis=-1)
    k = jnp.concatenate([k[..., :NOPE_DIM], apply_rope(k[..., NOPE_DIM:], cos, sin)], axis=-1)
    scale = QK_HEAD ** -0.5
    outs = []
    for qb in range(S // QBLOCK):
        q0 = qb * QBLOCK
        kend = q0 + QBLOCK
        qblk = q[:, q0:kend]
        s = jnp.einsum('bqhd,bkhd->bhqk', qblk, k[:, :kend],
                       preferred_element_type=jnp.float32) * scale
        qi = q0 + jnp.arange(QBLOCK)
        kj = jnp.arange(kend)
        allowed = (kj[None, :] // CHUNK) <= (qi[:, None] // CHUNK)
        s = jnp.where(allowed[None, None], s, NEG_INF)
        p = jax.nn.softmax(s, axis=-1)
        outs.append(jnp.einsum('bhqk,bkhd->bqhd', p.astype(v.dtype), v[:, :kend]))
    o = jnp.concatenate(outs, axis=1).reshape(B, S, N_HEADS * V_DIM)
    return o @ w_out


def pool_mixer(h, w_in, w_group, scale, w_out):
    B, S, D = h.shape
    u = (h @ w_in).reshape(B, S, N_POOL_GROUPS, POOL_GROUP_DIM)
    uf = u.astype(jnp.float32)
    cs = jnp.concatenate(
        [jnp.zeros((B, 1, N_POOL_GROUPS, POOL_GROUP_DIM), jnp.float32), jnp.cumsum(uf, axis=1)],
        axis=1)
    t = jnp.arange(S)
    pooled = []
    for gi, w in enumerate(POOL_WINDOWS):
        lo = jnp.maximum(t + 1 - w, 0)
        cnt = jnp.minimum(t + 1, w).astype(jnp.float32)
        cg = cs[:, :, gi]
        pooled.append((cg[:, 1:] - cg[:, lo]) / cnt[None, :, None])
    p = jnp.stack(pooled, axis=2) - uf
    y = jnp.einsum('bsgc,gcd->bsgd', p.astype(h.dtype), w_group).reshape(B, S, D) * scale
    return y @ w_out


def hier_moe(h2, w_rg, b_rg, w_re, b_re, w_gate, w_up, w_down):
    T, D = h2.shape
    lg = jnp.einsum('td,dg->tg', h2, w_rg).astype(jnp.float32) + b_rg.astype(jnp.float32)
    pg = jax.nn.softmax(lg, axis=-1)
    g_sel = jnp.argmax(pg, axis=-1)
    p_sel = jnp.take_along_axis(pg, g_sel[:, None], axis=-1)[:, 0]
    le = jnp.einsum('td,de->te', h2, w_re).astype(jnp.float32).reshape(
        T, N_GROUPS, EXPERTS_PER_GROUP) + b_re.astype(jnp.float32)[None]
    le_sel = jnp.take_along_axis(le, g_sel[:, None, None], axis=1)[:, 0]
    qv, qi = lax.top_k(jax.nn.softmax(le_sel, axis=-1), TOP_K_IN_GROUP)
    qv = qv / jnp.sum(qv, axis=-1, keepdims=True)
    gate_w = p_sel[:, None] * qv
    eid = g_sel[:, None] * EXPERTS_PER_GROUP + qi
    A = T * TOP_K_IN_GROUP
    e_flat = eid.reshape(-1).astype(jnp.int32)
    tok_flat = jnp.repeat(jnp.arange(T, dtype=jnp.int32), TOP_K_IN_GROUP)
    w_flat = gate_w.reshape(-1)
    order = jnp.argsort(e_flat)
    e_s, tok_s, w_s = e_flat[order], tok_flat[order], w_flat[order]
    counts = jnp.bincount(e_flat, length=N_EXPERTS)
    starts = jnp.cumsum(counts) - counts
    padded = ((counts + EXPERT_BLOCK - 1) // EXPERT_BLOCK) * EXPERT_BLOCK
    pends = jnp.cumsum(padded)
    pstarts = pends - padded
    dest = pstarts[e_s] + (jnp.arange(A) - starts[e_s])
    P = A + N_EXPERTS * EXPERT_BLOCK
    n_blocks = P // EXPERT_BLOCK
    buf_tok = jnp.full((P,), T, jnp.int32).at[dest].set(tok_s)
    buf_w = jnp.zeros((P,), jnp.float32).at[dest].set(w_s)
    block_exp = jnp.minimum(
        jnp.searchsorted(pends, jnp.arange(n_blocks) * EXPERT_BLOCK, side='right'),
        N_EXPERTS - 1)
    h_pad = jnp.concatenate([h2, jnp.zeros((1, D), h2.dtype)], axis=0)

    def run_block(args):
        tok, e = args
        xb = h_pad[tok]
        return (jax.nn.silu(xb @ w_gate[e]) * (xb @ w_up[e])) @ w_down[e]

    yb = lax.map(run_block, (buf_tok.reshape(n_blocks, EXPERT_BLOCK), block_exp))
    y = yb.reshape(P, D) * buf_w[:, None].astype(yb.dtype)
    return jax.ops.segment_sum(y, buf_tok, num_segments=T + 1)[:T]


def setup_inputs(seed: int = 0) -> dict:
    key = jax.random.key(seed)
    ks = iter(jax.random.split(key, 32))
    f32 = jnp.float32

    def nrm(shape, fan_in):
        return jax.random.normal(next(ks), shape, f32) * (fan_in ** -0.5)

    def gain(shape):
        return 1.0 + 0.02 * jax.random.normal(next(ks), shape, f32)

    x = jax.random.normal(next(ks), (BATCH, SEQ, D_MODEL), f32)
    offs = jax.random.randint(next(ks), (BATCH, 1), 0, 16384, dtype=jnp.int32)
    positions = offs + jnp.arange(SEQ, dtype=jnp.int32)[None, :]
    na, npl = N_MLA_LAYERS, N_POOL_LAYERS
    return {
        "x": x,
        "positions": positions,
        "mix_norm": gain((DEPTH, D_MODEL)),
        "mla_w_in": nrm((na, D_MODEL, MLA_IN), D_MODEL),
        "mla_q_lat_norm": gain((na, Q_LORA)),
        "mla_kv_lat_norm": gain((na, KV_LORA)),
        "mla_w_q_up": nrm((na, Q_LORA, N_HEADS * QK_HEAD), Q_LORA),
        "mla_w_kv_up": nrm((na, KV_LORA, N_HEADS * (NOPE_DIM + V_DIM)), KV_LORA),
        "mla_q_norm": gain((na, QK_HEAD)),
        "mla_k_norm": gain((na, QK_HEAD)),
        "mla_w_out": nrm((na, N_HEADS * V_DIM, D_MODEL), N_HEADS * V_DIM),
        "pool_w_in": nrm((npl, D_MODEL, D_MODEL), D_MODEL),
        "pool_w_group": nrm((npl, N_POOL_GROUPS, POOL_GROUP_DIM, POOL_GROUP_DIM), POOL_GROUP_DIM),
        "pool_scale": gain((npl, D_MODEL)),
        "pool_w_out": nrm((npl, D_MODEL, D_MODEL), D_MODEL),
        "ffn_norm": gain((DEPTH, D_MODEL)),
        "moe_w_router_group": nrm((DEPTH, D_MODEL, N_GROUPS), D_MODEL),
        "moe_b_router_group": 0.01 * jax.random.normal(next(ks), (DEPTH, N_GROUPS), f32),
        "moe_w_router_expert": nrm((DEPTH, D_MODEL, N_EXPERTS), D_MODEL),
        "moe_b_router_expert": 0.01 * jax.random.normal(next(ks), (DEPTH, N_GROUPS, EXPERTS_PER_GROUP), f32),
        "moe_w_gate": nrm((DEPTH, N_EXPERTS, D_MODEL, D_EXPERT), D_MODEL),
        "moe_w_up": nrm((DEPTH, N_EXPERTS, D_MODEL, D_EXPERT), D_MODEL),
        "moe_w_down": nrm((DEPTH, N_EXPERTS, D_EXPERT, D_MODEL), D_EXPERT),
    }


def reference(x, positions, mix_norm, mla_w_in, mla_q_lat_norm, mla_kv_lat_norm, mla_w_q_up,
              mla_w_kv_up, mla_q_norm, mla_k_norm, mla_w_out, pool_w_in, pool_w_group, pool_scale,
              pool_w_out, ffn_norm, moe_w_router_group, moe_b_router_group, moe_w_router_expert,
              moe_b_router_expert, moe_w_gate, moe_w_up, moe_w_down):
    B, S, D = x.shape
    for layer in range(DEPTH):
        h = rms_norm(x, mix_norm[layer])
        if layer % N_MIXERS == 0:
            a = layer // N_MIXERS
            mixed = mla_mixer(h, positions, mla_w_in[a], mla_q_lat_norm[a], mla_kv_lat_norm[a],
                              mla_w_q_up[a], mla_w_kv_up[a], mla_q_norm[a], mla_k_norm[a],
                              mla_w_out[a])
        else:
            p = layer // N_MIXERS
            mixed = pool_mixer(h, pool_w_in[p], pool_w_group[p], pool_scale[p], pool_w_out[p])
        x = x + mixed
        h = rms_norm(x, ffn_norm[layer])
        y = hier_moe(h.reshape(B * S, D), moe_w_router_group[layer], moe_b_router_group[layer],
                     moe_w_router_expert[layer], moe_b_router_expert[layer], moe_w_gate[layer],
                     moe_w_up[layer], moe_w_down[layer])
        x = x + y.reshape(B, S, D)
    return x
```

```python
import functools

import jax
import jax.numpy as jnp
from jax import lax
from jax.experimental import pallas as pl
from jax.experimental.pallas import tpu as pltpu

N_HEADS = 16
NOPE_DIM = 128
ROPE_DIM = 64
V_DIM = 128
QK_HEAD = NOPE_DIM + ROPE_DIM
ROPE_THETA = 10000.0
CHUNK = 64
CHUNK_SHIFT = CHUNK.bit_length() - 1
POOL_WINDOWS = (2, 4, 8, 16)
N_GROUPS = 8
EXPERTS_PER_GROUP = 8
EXPERT_BLOCK = 256
EPS = 1e-6
NEG_INF = -1e30

LANES = 128
QK_PAD = 2 * LANES
VMEM_LIMIT_BYTES = 56 * 1024 * 1024
POOL_HALO = 16

F32 = jnp.float32
BF16 = jnp.bfloat16


def _params(*sem):
    return pltpu.CompilerParams(dimension_semantics=sem, vmem_limit_bytes=VMEM_LIMIT_BYTES)


def _const_spec(shape):
    return pl.BlockSpec(shape, lambda *_: (0,) * len(shape))


def _rms(x, gain):
    ms = jnp.mean(x * x, axis=-1, keepdims=True)
    return x * lax.rsqrt(ms + EPS) * gain


def _mla_prep_kernel(x_ref, cos_ref, sin_ref, g_ref, wiq_ref, wikv_ref, wir_ref, gq_ref, gkv_ref,
                     wqn_ref, wqr_ref, wkn_ref, wv_ref, qgn_ref, qgr_ref, kgn_ref, kgr_ref,
                     q_ref, k_ref, v_ref):
    h = _rms(x_ref[...], g_ref[...]).astype(BF16)
    cq = jnp.dot(h, wiq_ref[...], preferred_element_type=F32)
    ckv = jnp.dot(h, wikv_ref[...], preferred_element_type=F32)
    kr = jnp.dot(h, wir_ref[...], preferred_element_type=F32)[:, :ROPE_DIM]
    cq = _rms(cq, gq_ref[...]).astype(BF16)
    ckv = _rms(ckv, gkv_ref[...]).astype(BF16)
    qn = jnp.dot(cq, wqn_ref[...], preferred_element_type=F32)
    qr = jnp.dot(cq, wqr_ref[...], preferred_element_type=F32)
    kn = jnp.dot(ckv, wkn_ref[...], preferred_element_type=F32)
    v = jnp.dot(ckv, wv_ref[...], preferred_element_type=F32)

    cosf = cos_ref[...]
    sinf = sin_ref[...]
    half = ROPE_DIM // 2

    def rope(t):
        rot = jnp.concatenate([t[:, half:], t[:, :half]], axis=-1)
        return t * cosf + rot * sinf

    tm = x_ref.shape[0]
    zpad = jnp.zeros((tm, QK_PAD - QK_HEAD), BF16)
    kr_ss = jnp.sum(kr * kr, axis=-1, keepdims=True)
    kr_rot = rope(kr * kgr_ref[...])
    for hd in range(N_HEADS):
        qn_h = qn[:, hd * NOPE_DIM:(hd + 1) * NOPE_DIM]
        qr_h = qr[:, hd * ROPE_DIM:(hd + 1) * ROPE_DIM]
        ss = jnp.sum(qn_h * qn_h, axis=-1, keepdims=True) + jnp.sum(qr_h * qr_h, axis=-1, keepdims=True)
        r = lax.rsqrt(ss * (1.0 / QK_HEAD) + EPS)
        q_ref[hd, :, 0:NOPE_DIM] = (qn_h * r * qgn_ref[...]).astype(BF16)
        q_ref[hd, :, NOPE_DIM:QK_HEAD] = (rope(qr_h * qgr_ref[...]) * r).astype(BF16)
        q_ref[hd, :, QK_HEAD:QK_PAD] = zpad

        kn_h = kn[:, hd * NOPE_DIM:(hd + 1) * NOPE_DIM]
        ss = jnp.sum(kn_h * kn_h, axis=-1, keepdims=True) + kr_ss
        r = lax.rsqrt(ss * (1.0 / QK_HEAD) + EPS)
        k_ref[hd, :, 0:NOPE_DIM] = (kn_h * r * kgn_ref[...]).astype(BF16)
        k_ref[hd, :, NOPE_DIM:QK_HEAD] = (kr_rot * r).astype(BF16)
        k_ref[hd, :, QK_HEAD:QK_PAD] = zpad

        v_ref[hd] = v[:, hd * V_DIM:(hd + 1) * V_DIM].astype(BF16)


def _mla_prep(x2, cosf, sinf, g, wiq, wikv, wir, gq, gkv, wqn, wqr, wkn, wv, qgn, qgr, kgn, kgr,
              batch, seq, tm):
    T, D = x2.shape
    n_s = seq // tm
    row = lambda b, i: (b * n_s + i, 0)
    consts = [g, wiq, wikv, wir, gq, gkv, wqn, wqr, wkn, wv, qgn, qgr, kgn, kgr]
    head_out = lambda width: pl.BlockSpec((None, N_HEADS, tm, width), lambda b, i: (b, 0, i, 0))
    return pl.pallas_call(
        _mla_prep_kernel,
        grid=(batch, n_s),
        in_specs=[pl.BlockSpec((tm, D), row), pl.BlockSpec((tm, ROPE_DIM), row),
                  pl.BlockSpec((tm, ROPE_DIM), row)] + [_const_spec(c.shape) for c in consts],
        out_specs=[head_out(QK_PAD), head_out(QK_PAD), head_out(V_DIM)],
        out_shape=[jax.ShapeDtypeStruct((batch, N_HEADS, seq, QK_PAD), BF16),
                   jax.ShapeDtypeStruct((batch, N_HEADS, seq, QK_PAD), BF16),
                   jax.ShapeDtypeStruct((batch, N_HEADS, seq, V_DIM), BF16)],
        compiler_params=_params("parallel", "parallel"),
        name="mla_prep",
    )(x2, cosf, sinf, *consts)


def _attn_kernel(q_ref, k_ref, v_ref, o_ref, *, tile):
    qi = pl.program_id(2)
    q = q_ref[...]

    def step(j, carry, masked):
        m, l, acc = carry
        start = pl.multiple_of(j * tile, tile)
        k = k_ref[pl.ds(start, tile), :]
        v = v_ref[pl.ds(start, tile), :]
        s = lax.dot_general(q, k, (((1,), (1,)), ((), ())), preferred_element_type=F32)
        if masked:
            rows = lax.broadcasted_iota(jnp.int32, s.shape, 0) >> CHUNK_SHIFT
            cols = lax.broadcasted_iota(jnp.int32, s.shape, 1) >> CHUNK_SHIFT
            s = jnp.where(cols <= rows, s, NEG_INF)
        m_new = jnp.maximum(m, jnp.max(s, axis=-1, keepdims=True))
        alpha = jnp.exp(m - m_new)
        p = jnp.exp(s - m_new)
        l = alpha * l + jnp.sum(p, axis=-1, keepdims=True)
        acc = alpha * acc + jnp.dot(p.astype(BF16), v, preferred_element_type=F32)
        return m_new, l, acc

    init = (jnp.full((tile, 1), NEG_INF, F32), jnp.zeros((tile, 1), F32), jnp.zeros((tile, V_DIM), F32))
    carry = lax.fori_loop(0, qi, lambda j, c: step(j, c, False), init)
    _, l, acc = step(qi, carry, True)
    o_ref[...] = (acc / l).astype(o_ref.dtype)


def _attention(q, k, v, tile):
    B, H, S, _ = q.shape
    return pl.pallas_call(
        functools.partial(_attn_kernel, tile=tile),
        grid=(B, H, S // tile),
        in_specs=[pl.BlockSpec((None, None, tile, QK_PAD), lambda b, h, i: (b, h, i, 0)),
                  pl.BlockSpec((None, None, S, QK_PAD), lambda b, h, i: (b, h, 0, 0)),
                  pl.BlockSpec((None, None, S, V_DIM), lambda b, h, i: (b, h, 0, 0))],
        out_specs=pl.BlockSpec((None, tile, V_DIM), lambda b, h, i: (b, i, h)),
        out_shape=jax.ShapeDtypeStruct((B, S, H * V_DIM), BF16),
        compiler_params=_params("parallel", "parallel", "arbitrary"),
        name="mla_attention",
    )(q, k, v)


def _route(logits):
    lane = lax.broadcasted_iota(jnp.int32, logits.shape, 1)
    is_g = lane < N_GROUPS
    lg = jnp.where(is_g, logits, NEG_INF)
    mg = jnp.max(lg, axis=-1, keepdims=True)
    g_sel = jnp.min(jnp.where(lg == mg, lane, LANES), axis=-1, keepdims=True)
    p_sel = 1.0 / jnp.sum(jnp.where(is_g, jnp.exp(lg - mg), 0.0), axis=-1, keepdims=True)
    e_lane = lane - N_GROUPS
    e_lo = g_sel * EXPERTS_PER_GROUP
    in_grp = (e_lane >= e_lo) & (e_lane < e_lo + EXPERTS_PER_GROUP)
    le = jnp.where(in_grp, logits, NEG_INF)
    m1 = jnp.max(le, axis=-1, keepdims=True)
    i1 = jnp.min(jnp.where(le == m1, lane, LANES), axis=-1, keepdims=True)
    le2 = jnp.where(lane == i1, NEG_INF, le)
    m2 = jnp.max(le2, axis=-1, keepdims=True)
    i2 = jnp.min(jnp.where(le2 == m2, lane, LANES), axis=-1, keepdims=True)
    e2 = jnp.exp(m2 - m1)
    w1 = p_sel / (1.0 + e2)
    w2 = p_sel * e2 / (1.0 + e2)
    ids = jnp.where(lane == 0, i1 - N_GROUPS, jnp.where(lane == 1, i2 - N_GROUPS, 0))
    wts = jnp.where(lane == 0, w1, jnp.where(lane == 1, w2, 0.0))
    return ids, wts


def _proj_route_kernel(a_ref, w_ref, x_ref, g_ref, wr_ref, br_ref, x1_ref, h2_ref, ids_ref, wts_ref):
    x1 = x_ref[...] + jnp.dot(a_ref[...], w_ref[...], preferred_element_type=F32)
    x1_ref[...] = x1
    h2 = _rms(x1, g_ref[...])
    h2_ref[...] = h2
    logits = jnp.dot(h2.astype(BF16), wr_ref[...], preferred_element_type=F32) + br_ref[...]
    ids, wts = _route(logits)
    ids_ref[...] = ids
    wts_ref[...] = wts


def _proj_route(a, w, x2, g, wr, br, tm):
    T, D = x2.shape
    K = a.shape[1]
    row = lambda i: (i, 0)
    return pl.pallas_call(
        _proj_route_kernel,
        grid=(T // tm,),
        in_specs=[pl.BlockSpec((tm, K), row), _const_spec(w.shape), pl.BlockSpec((tm, D), row),
                  _const_spec(g.shape), _const_spec(wr.shape), _const_spec(br.shape)],
        out_specs=[pl.BlockSpec((tm, D), row), pl.BlockSpec((tm, D), row),
                   pl.BlockSpec((tm, LANES), row), pl.BlockSpec((tm, LANES), row)],
        out_shape=[jax.ShapeDtypeStruct((T, D), F32), jax.ShapeDtypeStruct((T, D), F32),
                   jax.ShapeDtypeStruct((T, LANES), jnp.int32), jax.ShapeDtypeStruct((T, LANES), F32)],
        compiler_params=_params("parallel"),
        name="proj_route",
    )(a, w, x2, g, wr, br)


def _row_copy(src, src_row, dst, dst_row, sem):
    return pltpu.make_async_copy(src.at[pl.ds(src_row, 1)], dst.at[pl.ds(dst_row, 1)], sem)


def _gather_kernel(tok_ref, nused_ref, h_hbm, o_ref, sem):
    b = pl.program_id(0)
    rows = o_ref.shape[0]

    @pl.when(b < nused_ref[0])
    def _():
        def issue(i, c):
            _row_copy(h_hbm, tok_ref[b * rows + i], o_ref, i, sem).start()
            return c
        lax.fori_loop(0, rows, issue, 0, unroll=8)
        pltpu.make_async_copy(h_hbm.at[pl.ds(0, rows)], o_ref, sem).wait()

    @pl.when(b >= nused_ref[0])
    def _():
        o_ref[...] = jnp.zeros_like(o_ref)


def _gather_rows(slot_tok, n_used, h2, n_blocks):
    T, D = h2.shape
    return pl.pallas_call(
        _gather_kernel,
        grid_spec=pltpu.PrefetchScalarGridSpec(
            num_scalar_prefetch=2,
            grid=(n_blocks,),
            in_specs=[pl.BlockSpec(memory_space=pl.ANY)],
            out_specs=pl.BlockSpec((EXPERT_BLOCK, D), lambda b, *_: (b, 0)),
            scratch_shapes=[pltpu.SemaphoreType.DMA(())]),
        out_shape=jax.ShapeDtypeStruct((n_blocks * EXPERT_BLOCK, D), F32),
        compiler_params=_params("arbitrary"),
        name="moe_gather",
    )(slot_tok, n_used, h2)


def _expert_kernel(bexp_ref, nused_ref, x_ref, wg_ref, wu_ref, wd_ref, o_ref, wg_s, wu_s, wd_s):
    b = pl.program_id(0)
    prev = bexp_ref[jnp.maximum(b - 1, 0)]

    @pl.when((b == 0) | (bexp_ref[b] != prev))
    def _():
        wg_s[...] = wg_ref[...].astype(BF16)
        wu_s[...] = wu_ref[...].astype(BF16)
        wd_s[...] = wd_ref[...].astype(BF16)

    @pl.when(b < nused_ref[0])
    def _():
        x = x_ref[...].astype(BF16)
        gate = jnp.dot(x, wg_s[...], preferred_element_type=F32)
        up = jnp.dot(x, wu_s[...], preferred_element_type=F32)
        act = (gate * jax.nn.sigmoid(gate) * up).astype(BF16)
        o_ref[...] = jnp.dot(act, wd_s[...], preferred_element_type=F32)

    @pl.when(b >= nused_ref[0])
    def _():
        o_ref[...] = jnp.zeros_like(o_ref)


def _expert_mlp(block_exp, n_used, xs, w_gate, w_up, w_down):
    P, D = xs.shape
    n_e, _, DE = w_gate.shape
    n_blocks = P // EXPERT_BLOCK
    return pl.pallas_call(
        _expert_kernel,
        grid_spec=pltpu.PrefetchScalarGridSpec(
            num_scalar_prefetch=2,
            grid=(n_blocks,),
            in_specs=[pl.BlockSpec((EXPERT_BLOCK, D), lambda b, be, nu: (b, 0)),
                      pl.BlockSpec((None, D, DE), lambda b, be, nu: (be[b], 0, 0)),
                      pl.BlockSpec((None, D, DE), lambda b, be, nu: (be[b], 0, 0)),
                      pl.BlockSpec((None, DE, D), lambda b, be, nu: (be[b], 0, 0))],
            out_specs=pl.BlockSpec((EXPERT_BLOCK, D), lambda b, be, nu: (b, 0)),
            scratch_shapes=[pltpu.VMEM((D, DE), BF16), pltpu.VMEM((D, DE), BF16),
                            pltpu.VMEM((DE, D), BF16)]),
        out_shape=jax.ShapeDtypeStruct((P, D), F32),
        compiler_params=_params("arbitrary"),
        name="moe_experts",
    )(block_exp, n_used, xs, w_gate, w_up, w_down)


def _combine_kernel(dest_ref, ys_hbm, x_ref, wts_ref, o_ref, buf, sem):
    i = pl.program_id(0)
    tc = x_ref.shape[0]

    def issue(t, c):
        base = 2 * (i * tc + t)
        _row_copy(ys_hbm, dest_ref[base], buf.at[0], t, sem).start()
        _row_copy(ys_hbm, dest_ref[base + 1], buf.at[1], t, sem).start()
        return c
    lax.fori_loop(0, tc, issue, 0, unroll=4)
    pltpu.make_async_copy(ys_hbm.at[pl.ds(0, tc)], buf.at[0], sem).wait()
    pltpu.make_async_copy(ys_hbm.at[pl.ds(0, tc)], buf.at[1], sem).wait()
    wts = wts_ref[...]
    o_ref[...] = x_ref[...] + (wts[:, 0:1] * buf[0] + wts[:, 1:2] * buf[1])


def _combine(dest, ys, x2, wts, tc):
    T, D = x2.shape
    row = lambda i, *_: (i, 0)
    return pl.pallas_call(
        _combine_kernel,
        grid_spec=pltpu.PrefetchScalarGridSpec(
            num_scalar_prefetch=1,
            grid=(T // tc,),
            in_specs=[pl.BlockSpec(memory_space=pl.ANY), pl.BlockSpec((tc, D), row),
                      pl.BlockSpec((tc, LANES), row)],
            out_specs=pl.BlockSpec((tc, D), row),
            scratch_shapes=[pltpu.VMEM((2, tc, D), F32), pltpu.SemaphoreType.DMA(())]),
        out_shape=jax.ShapeDtypeStruct((T, D), F32),
        compiler_params=_params("arbitrary"),
        name="moe_combine",
    )(dest, ys, x2, wts)


def _dispatch_plan(ids):
    T = ids.shape[0]
    n_e = N_GROUPS * EXPERTS_PER_GROUP
    A = 2 * T
    n_blocks = A // EXPERT_BLOCK + n_e
    e_flat = ids.reshape(A)
    seg = EXPERT_BLOCK
    onehot = (e_flat.reshape(A // seg, seg, 1) == jnp.arange(n_e, dtype=jnp.int32)).astype(F32)
    tril = jnp.tril(jnp.ones((seg, seg), F32))
    within = jnp.einsum('ij,bjk->bik', tril, onehot)
    seg_tot = within[:, -1, :]
    seg_off = jnp.cumsum(seg_tot, axis=0) - seg_tot
    counts = (seg_off[-1] + seg_tot[-1]).astype(jnp.int32)
    padded = ((counts + EXPERT_BLOCK - 1) // EXPERT_BLOCK) * EXPERT_BLOCK
    pends = jnp.cumsum(padded)
    pstarts = pends - padded
    pos = within - 1.0 + seg_off[:, None, :] + pstarts.astype(F32)[None, None, :]
    dest = jnp.sum(onehot * pos, axis=-1).reshape(A).astype(jnp.int32)
    tok = jnp.arange(A, dtype=jnp.int32) // 2
    slot_tok = jnp.zeros((n_blocks * EXPERT_BLOCK,), jnp.int32).at[dest].set(tok)
    block_start = jnp.arange(n_blocks, dtype=jnp.int32) * EXPERT_BLOCK
    block_exp = jnp.minimum(jnp.sum(pends[None, :] <= block_start[:, None], axis=1), n_e - 1)
    n_used = (pends[-1] // EXPERT_BLOCK).reshape(1)
    return slot_tok, dest, block_exp.astype(jnp.int32), n_used.astype(jnp.int32)


def _moe(x1, h2, ids, wts, w_gate, w_up, w_down):
    slot_tok, dest, block_exp, n_used = _dispatch_plan(ids[:, :2])
    n_blocks = block_exp.shape[0]
    xs = _gather_rows(slot_tok, n_used, h2, n_blocks)
    ys = _expert_mlp(block_exp, n_used, xs, w_gate, w_up, w_down)
    return _combine(dest, ys, x1, wts, tc=128)


def _pool_in_kernel(x_ref, g_ref, w_ref, p_ref, ext_ref):
    i = pl.program_id(1)
    tm = x_ref.shape[0]
    D = w_ref.shape[1]
    gdim = D // len(POOL_WINDOWS)

    @pl.when(i == 0)
    def _():
        ext_ref[0:POOL_HALO, :] = jnp.zeros((POOL_HALO, D), F32)

    h = _rms(x_ref[...], g_ref[...]).astype(BF16)
    ext_ref[POOL_HALO:, :] = jnp.dot(h, w_ref[...], preferred_element_type=F32)
    t = i * tm + lax.broadcasted_iota(jnp.int32, (tm, 1), 0)
    for gi, win in enumerate(POOL_WINDOWS):
        cols = slice(gi * gdim, (gi + 1) * gdim)
        e = ext_ref[:, cols]
        s, span = e, 1
        while span < win:
            s = s + pltpu.roll(s, span, axis=0)
            span *= 2
        cnt = jnp.minimum(t + 1, win).astype(F32)
        u = e[POOL_HALO:, :]
        p_ref[:, cols] = (s[POOL_HALO:, :] / cnt - u).astype(p_ref.dtype)
    ext_ref[0:POOL_HALO, :] = ext_ref[tm:tm + POOL_HALO, :]


def _pool_in(x2, g, w, batch, seq, tm):
    T, D = x2.shape
    n_s = seq // tm
    row = lambda b, i: (b * n_s + i, 0)
    return pl.pallas_call(
        _pool_in_kernel,
        grid=(batch, n_s),
        in_specs=[pl.BlockSpec((tm, D), row), _const_spec(g.shape), _const_spec(w.shape)],
        out_specs=pl.BlockSpec((tm, D), row),
        out_shape=jax.ShapeDtypeStruct((T, D), BF16),
        scratch_shapes=[pltpu.VMEM((POOL_HALO + tm, D), F32)],
        compiler_params=_params("parallel", "arbitrary"),
        name="pool_in",
    )(x2, g, w)


def _pool_group_kernel(p_ref, w_ref, s_ref, o_ref):
    y = jnp.dot(p_ref[...], w_ref[...], preferred_element_type=F32)
    o_ref[...] = (y * s_ref[...]).astype(o_ref.dtype)


def _pool_group(p, wg, scale, tm):
    T, D = p.shape
    G, C, _ = wg.shape
    return pl.pallas_call(
        _pool_group_kernel,
        grid=(T // tm, G),
        in_specs=[pl.BlockSpec((tm, C), lambda i, g: (i, g)),
                  pl.BlockSpec((None, C, C), lambda i, g: (g, 0, 0)),
                  pl.BlockSpec((1, C), lambda i, g: (0, g))],
        out_specs=pl.BlockSpec((tm, C), lambda i, g: (i, g)),
        out_shape=jax.ShapeDtypeStruct((T, D), BF16),
        compiler_params=_params("parallel", "parallel"),
        name="pool_group",
    )(p, wg, scale)


def _router_weights(w_rg, b_rg, w_re, b_re):
    D = w_rg.shape[0]
    n_e = N_GROUPS * EXPERTS_PER_GROUP
    pad = LANES - N_GROUPS - n_e
    wr = jnp.concatenate([w_rg, w_re, jnp.zeros((D, pad), F32)], axis=1).astype(BF16)
    br = jnp.concatenate([b_rg, b_re.reshape(n_e), jnp.zeros((pad,), F32)]).reshape(1, LANES)
    return wr, br


def kernel(x, positions, mix_norm, mla_w_in, mla_q_lat_norm, mla_kv_lat_norm, mla_w_q_up, mla_w_kv_up, mla_q_norm, mla_k_norm, mla_w_out, pool_w_in, pool_w_group, pool_scale, pool_w_out, ffn_norm, moe_w_router_group, moe_b_router_group, moe_w_router_expert, moe_b_router_expert, moe_w_gate, moe_w_up, moe_w_down):
    B, S, D = x.shape
    T = B * S
    depth = mix_norm.shape[0]
    q_lora = mla_q_lat_norm.shape[1]
    kv_lora = mla_kv_lat_norm.shape[1]
    x2 = x.reshape(T, D)

    inv_freq = 1.0 / (ROPE_THETA ** (jnp.arange(0, ROPE_DIM, 2, dtype=F32) / ROPE_DIM))
    ang = positions.astype(F32).reshape(T, 1) * inv_freq[None, :]
    cos, sin = jnp.cos(ang), jnp.sin(ang)
    cosf = jnp.concatenate([cos, cos], axis=-1)
    sinf = jnp.concatenate([-sin, sin], axis=-1)

    for layer in range(depth):
        g_mix = mix_norm[layer].reshape(1, D)
        if layer % 2 == 0:
            a = layer // 2
            w_in = mla_w_in[a].astype(BF16)
            wiq = w_in[:, :q_lora]
            wikv = w_in[:, q_lora:q_lora + kv_lora]
            wir = jnp.pad(w_in[:, q_lora + kv_lora:], ((0, 0), (0, LANES - ROPE_DIM)))
            wq = mla_w_q_up[a].astype(BF16).reshape(q_lora, N_HEADS, QK_HEAD)
            wqn = wq[:, :, :NOPE_DIM].reshape(q_lora, N_HEADS * NOPE_DIM)
            wqr = wq[:, :, NOPE_DIM:].reshape(q_lora, N_HEADS * ROPE_DIM)
            wkv = mla_w_kv_up[a].astype(BF16).reshape(kv_lora, N_HEADS, NOPE_DIM + V_DIM)
            wkn = wkv[:, :, :NOPE_DIM].reshape(kv_lora, N_HEADS * NOPE_DIM)
            wv = wkv[:, :, NOPE_DIM:].reshape(kv_lora, N_HEADS * V_DIM)
            qg = mla_q_norm[a] * (QK_HEAD ** -0.5)
            kg = mla_k_norm[a]
            q, k, v = _mla_prep(
                x2, cosf, sinf, g_mix, wiq, wikv, wir,
                mla_q_lat_norm[a].reshape(1, q_lora), mla_kv_lat_norm[a].reshape(1, kv_lora),
                wqn, wqr, wkn, wv,
                qg[:NOPE_DIM].reshape(1, NOPE_DIM), qg[NOPE_DIM:].reshape(1, ROPE_DIM),
                kg[:NOPE_DIM].reshape(1, NOPE_DIM), kg[NOPE_DIM:].reshape(1, ROPE_DIM),
                batch=B, seq=S, tm=min(256, S))
            mixed_in = _attention(q, k, v, tile=min(256, S)).reshape(T, N_HEADS * V_DIM)
            w_out = mla_w_out[a].astype(BF16)
        else:
            p = layer // 2
            pooled = _pool_in(x2, g_mix, pool_w_in[p].astype(BF16), batch=B, seq=S, tm=min(256, S))
            mixed_in = _pool_group(pooled, pool_w_group[p].astype(BF16), pool_scale[p].reshape(1, D),
                                   tm=min(512, T))
            w_out = pool_w_out[p].astype(BF16)
        wr, br = _router_weights(moe_w_router_group[layer], moe_b_router_group[layer],
                                 moe_w_router_expert[layer], moe_b_router_expert[layer])
        x1, h2, ids, wts = _proj_route(mixed_in, w_out, x2, ffn_norm[layer].reshape(1, D), wr, br,
                                       tm=min(256, T))
        x2 = _moe(x1, h2, ids, wts, moe_w_gate[layer], moe_w_up[layer], moe_w_down[layer])
    return x2.reshape(B, S, D)
```
